```python
import math
import jax, jax.numpy as jnp
from jax import lax
import numpy as np

D_MODEL = 2048
BATCH = 1
SEQ = 8192
DEPTH = 1
DEC_BATCH = 32
DEC_SEQ = 8
PAST_LEN = 16384
PAGE_SIZE = 128

HEAD_DIM = 64
D_RWKV = D_MODEL // 2
D_ATT = D_MODEL - D_RWKV
H_RWKV = D_RWKV // HEAD_DIM
H_ATT = D_ATT // HEAD_DIM
D_DECAY_LORA = max(32, int(round(1.8 * D_MODEL ** 0.5 / 32)) * 32)
D_AAA_LORA = max(32, int(round(1.8 * D_MODEL ** 0.5 / 32)) * 32)
D_GATE_LORA = max(32, int(round(0.6 * D_MODEL ** 0.8 / 32)) * 32)
GN_EPS = 64e-5
LN_EPS = 1e-5
DILATED_BRANCHES = ((128, 1), (512, 4), (2048, 16))
ATT_BLOCK = 128
WIN_MAX = max(w for w, _ in DILATED_BRANCHES)
PEER_HEADS = 8
PEER_TOPK = 16
N_KEYS = 128
N_EXPERTS = N_KEYS * N_KEYS
PEER_DK = 256
PEER_BLOCK = 128
DN_ALPHA = (2.0 * DEPTH) ** 0.25
DN_BETA = (8.0 * DEPTH) ** -0.25

kernel_name = "hymba_rwkv7_dilated_peer_step"


def layer_norm(x, g, b, eps=LN_EPS):
    xf = x.astype(jnp.float32)
    mu = jnp.mean(xf, axis=-1, keepdims=True)
    var = jnp.mean(jnp.square(xf - mu), axis=-1, keepdims=True)
    return ((xf - mu) * lax.rsqrt(var + eps) * g.astype(jnp.float32) + b.astype(jnp.float32)).astype(x.dtype)


def alibi_slopes(n_heads):
    return jnp.asarray([2.0 ** (-8.0 * (h + 1) / n_heads) for h in range(n_heads)], jnp.float32)


def wkv7_scan(S0, r, w, k, v, a, b):
    def step(S, inp):
        r_t, w_t, k_t, v_t, a_t, b_t = inp
        sa = jnp.einsum('bhvk,bhk->bhv', S, a_t)
        S = S * w_t[:, :, None, :] + sa[..., None] * b_t[:, :, None, :] + v_t[..., None] * k_t[:, :, None, :]
        y = jnp.einsum('bhvk,bhk->bhv', S, r_t)
        return S, y
    xs = tuple(jnp.transpose(t, (1, 0, 2, 3)) for t in (r, w, k, v, a, b))
    S, ys = lax.scan(step, S0, xs)
    return jnp.transpose(ys, (1, 0, 2, 3)), S


def rwkv7_and_qkv(x, prev_x, wkv0, w_in, mu_rkv, mu_x, w0, w1, w2, a0, a1, a2, g1, g2,
                  k_k, k_a, r_k, lnx_g, lnx_b):
    B, T, _ = x.shape
    f32 = jnp.float32
    n_rkv = 3 * D_RWKV
    proj = x @ w_in
    rkv = proj[..., :n_rkv]
    prev_rkv = prev_x @ w_in[:, :n_rkv]
    rkv_shift = jnp.concatenate([prev_rkv[:, None], rkv[:, :-1]], axis=1)
    rkv = rkv + (rkv_shift - rkv) * mu_rkv
    r, k, v = jnp.split(rkv, 3, axis=-1)
    xx = jnp.concatenate([prev_x[:, None], x[:, :-1]], axis=1) - x
    xw = x + xx * mu_x[0]
    xa = x + xx * mu_x[1]
    xg = x + xx * mu_x[2]
    w = -jax.nn.softplus(-(w0 + jnp.tanh(xw @ w1) @ w2).astype(f32)) - 0.5
    decay = jnp.exp(-jnp.exp(w))
    a = jax.nn.sigmoid((a0 + (xa @ a1) @ a2).astype(f32))
    g = (jax.nn.sigmoid(xg @ g1) @ g2).astype(f32)
    hd = lambda t: t.astype(f32).reshape(B, T, H_RWKV, HEAD_DIM)
    r_h, k_h, v_h, a_h, w_h = hd(r), hd(k), hd(v), hd(a), hd(decay)
    kk = k_h * k_k.astype(f32).reshape(H_RWKV, HEAD_DIM)
    kk = kk / jnp.maximum(jnp.sqrt(jnp.sum(kk * kk, axis=-1, keepdims=True)), 1e-12)
    k_h = k_h * (1.0 + (a_h - 1.0) * k_a.astype(f32).reshape(H_RWKV, HEAD_DIM))
    y, wkv = wkv7_scan(wkv0.astype(f32), r_h, w_h, k_h, v_h, -kk, kk * a_h)
    mu = jnp.mean(y, axis=-1, keepdims=True)
    var = jnp.mean(jnp.square(y - mu), axis=-1, keepdims=True)
    y = (y - mu) * lax.rsqrt(var + GN_EPS) * lnx_g.astype(f32).reshape(H_RWKV, HEAD_DIM) \
        + lnx_b.astype(f32).reshape(H_RWKV, HEAD_DIM)
    y = y + jnp.sum(r_h * k_h * r_k.astype(f32), axis=-1, keepdims=True) * v_h
    y = (y.reshape(B, T, D_RWKV) * g).astype(x.dtype)
    qa, ka, va = jnp.split(proj[..., n_rkv:], 3, axis=-1)
    hs = lambda t: t.reshape(B, T, H_ATT, HEAD_DIM)
    return y, wkv, hs(qa), hs(ka), hs(va)


def combine_branches(outs, lses, dtype):
    wts = jax.nn.softmax(jnp.stack(lses, 0), axis=0)
    return jnp.einsum('cbth,cbthd->bthd', wts.astype(dtype), jnp.stack(outs, 0))


def dilated_attention_prompt(q, k, v):
    B, S, H, Dh = q.shape
    slopes = alibi_slopes(H)
    scale = Dh ** -0.5
    qi = jnp.arange(ATT_BLOCK)
    ki = jnp.arange(2 * ATT_BLOCK)
    jdist = ATT_BLOCK + qi[:, None] - ki[None, :]
    outs, lses = [], []
    for window, dil in DILATED_BRANCHES:
        span = window // dil
        unit = dil * ATT_BLOCK
        s_pad = -(-S // unit) * unit
        A = s_pad // dil
        nb = A // ATT_BLOCK

        def split(t):
            t = jnp.pad(t, ((0, 0), (0, s_pad - S), (0, 0), (0, 0)))
            t = jnp.transpose(t.reshape(B, A, dil, H, Dh), (0, 2, 1, 3, 4))
            return t.reshape(B, dil, nb, ATT_BLOCK, H, Dh)

        def with_prev(t):
            prev = jnp.pad(t, ((0, 0), (0, 0), (1, 0), (0, 0), (0, 0), (0, 0)))[:, :, :-1]
            return jnp.concatenate([prev, t], axis=3)

        qb = split(q)
        kw, vw = with_prev(split(k)), with_prev(split(v))
        s = jnp.einsum('brnqhd,brnkhd->brnqhk', qb, kw).astype(jnp.float32) * scale
        band = (jdist >= 0) & (jdist <= span)
        first = (jnp.arange(nb)[:, None, None] == 0) & (ki[None, None, :] < ATT_BLOCK)
        valid = band[None] & ~first
        bias = -slopes[None, :, None] * (jdist * dil).astype(jnp.float32)[:, None, :]
        s = jnp.where(valid[None, None, :, :, None, :], s + bias, -jnp.inf)
        lse = jax.nn.logsumexp(s, axis=-1)
        p = jnp.exp(s - lse[..., None])
        o = jnp.einsum('brnqhk,brnkhd->brnqhd', p.astype(v.dtype), vw)

        def merge(t):
            t = t.reshape((B, dil, A) + t.shape[4:])
            t = jnp.moveaxis(t, 1, 2)
            return t.reshape((B, s_pad) + t.shape[3:])[:, :S]

        outs.append(merge(o))
        lses.append(merge(lse))
    return combine_branches(outs, lses, v.dtype)


def dilated_attention_sample(q, k, v, cache_k, cache_v):
    B, T, H, Dh = q.shape
    WB = cache_k.shape[1]
    slopes = alibi_slopes(H)
    scale = Dh ** -0.5
    k_all = jnp.concatenate([cache_k.astype(k.dtype), k], axis=1)
    v_all = jnp.concatenate([cache_v.astype(v.dtype), v], axis=1)
    outs, lses = [], []
    for window, dil in DILATED_BRANCHES:
        span = window // dil
        jj = jnp.arange(span + 1)
        idx = WB + jnp.arange(T)[:, None] - jj[None, :] * dil
        valid = idx >= 0
        idxc = jnp.maximum(idx, 0)
        kg = k_all[:, idxc]
        vg = v_all[:, idxc]
        s = jnp.einsum('bthd,btjhd->bthj', q, kg).astype(jnp.float32) * scale
        bias = -slopes[:, None] * (jj * dil).astype(jnp.float32)[None, :]
        s = jnp.where(valid[None, :, None, :], s + bias, -jnp.inf)
        lse = jax.nn.logsumexp(s, axis=-1)
        p = jnp.exp(s - lse[..., None])
        outs.append(jnp.einsum('bthj,btjhd->bthd', p.astype(v.dtype), vg))
        lses.append(lse)
    return combine_branches(outs, lses, v.dtype)


def peer_ffn(h, wq, k1, k2, u_tab, v_tab):
    B, T, D = h.shape
    n = B * T
    x = h.reshape(n, D)
    half = PEER_DK // 2
    q = (x @ wq).reshape(n, PEER_HEADS, PEER_DK)
    s1 = jnp.einsum('nhc,kc->nhk', q[..., :half], k1).astype(jnp.float32)
    s2 = jnp.einsum('nhc,kc->nhk', q[..., half:], k2).astype(jnp.float32)
    v1, i1 = lax.top_k(s1, PEER_TOPK)
    v2, i2 = lax.top_k(s2, PEER_TOPK)
    cand = (v1[..., :, None] + v2[..., None, :]).reshape(n, PEER_HEADS, PEER_TOPK * PEER_TOPK)
    cidx = (i1[..., :, None] * N_KEYS + i2[..., None, :]).reshape(n, PEER_HEADS, PEER_TOPK * PEER_TOPK)
    top_s, sel = lax.top_k(cand, PEER_TOPK)
    eidx = jnp.take_along_axis(cidx, sel, axis=-1)
    gate = jax.nn.softmax(top_s, axis=-1)
    nb = -(-n // PEER_BLOCK)
    pad = nb * PEER_BLOCK - n
    xb = jnp.pad(x, ((0, pad), (0, 0))).reshape(nb, PEER_BLOCK, D)
    eb = jnp.pad(eidx, ((0, pad), (0, 0), (0, 0))).reshape(nb, PEER_BLOCK, PEER_HEADS, PEER_TOPK)
    gb = jnp.pad(gate, ((0, pad), (0, 0), (0, 0))).reshape(nb, PEER_BLOCK, PEER_HEADS, PEER_TOPK)

    def expert_block(args):
        xt, et, gt = args
        u = u_tab[et]
        act = jax.nn.gelu(jnp.einsum('td,thkd->thk', xt, u).astype(jnp.float32), approximate=False) * gt
        return jnp.einsum('thk,thkd->td', act.astype(xt.dtype), v_tab[et])

    out = lax.map(expert_block, (xb, eb, gb))
    return out.reshape(nb * PEER_BLOCK, D)[:n].reshape(B, T, D)


def post_mix(x, y_rwkv, y_att, w_out, ln1_g, ln1_b, ln2_g, ln2_b, peer_wq, peer_k1, peer_k2, peer_u, peer_v):
    B, T, _ = x.shape
    mix = jnp.concatenate([y_rwkv, y_att.reshape(B, T, D_ATT)], axis=-1) @ w_out
    h = layer_norm(DN_ALPHA * x + mix, ln1_g, ln1_b)
    return layer_norm(DN_ALPHA * h + peer_ffn(h, peer_wq, peer_k1, peer_k2, peer_u, peer_v), ln2_g, ln2_b)


def setup_inputs(seed: int = 0) -> dict:
    key = jax.random.key(seed)
    ks = iter(jax.random.split(key, 40))
    f32 = jnp.float32
    nrm = lambda shape, s: jax.random.normal(next(ks), shape, f32) * s
    uni = lambda shape, lo, hi: jax.random.uniform(next(ks), shape, f32, lo, hi)
    L = DEPTH
    WB = min(WIN_MAX, PAST_LEN)
    d_in = 3 * D_RWKV + 3 * D_ATT
    d_mix = D_RWKV + D_ATT
    return {
        'x_prompt': nrm((BATCH, SEQ, D_MODEL), 1.0),
        'x_sample': nrm((DEC_BATCH, DEC_SEQ, D_MODEL), 1.0),
        'cache_k': nrm((L, DEC_BATCH, WB, H_ATT, HEAD_DIM), 1.0),
        'cache_v': nrm((L, DEC_BATCH, WB, H_ATT, HEAD_DIM), 1.0),
        'state_wkv': nrm((L, DEC_BATCH, H_RWKV, HEAD_DIM, HEAD_DIM), 0.1),
        'state_shift': nrm((L, DEC_BATCH, D_MODEL), 1.0),
        'ln_emb_g': 1.0 + nrm((D_MODEL,), 0.02),
        'ln_emb_b': nrm((D_MODEL,), 0.02),
        'w_in': nrm((L, D_MODEL, d_in), D_MODEL ** -0.5),
        'mu_rkv': uni((L, 3 * D_RWKV), 0.1, 0.9),
        'mu_x': uni((L, 3, D_MODEL), 0.1, 0.9),
        'w0': uni((L, D_RWKV), -6.0, 1.0),
        'w1': nrm((L, D_MODEL, D_DECAY_LORA), D_MODEL ** -0.5),
        'w2': nrm((L, D_DECAY_LORA, D_RWKV), 0.5 * D_DECAY_LORA ** -0.5),
        'a0': nrm((L, D_RWKV), 0.1),
        'a1': nrm((L, D_MODEL, D_AAA_LORA), D_MODEL ** -0.5),
        'a2': nrm((L, D_AAA_LORA, D_RWKV), 0.5 * D_AAA_LORA ** -0.5),
        'g1': nrm((L, D_MODEL, D_GATE_LORA), D_MODEL ** -0.5),
        'g2': nrm((L, D_GATE_LORA, D_RWKV), D_GATE_LORA ** -0.5),
        'k_k': 0.85 + nrm((L, D_RWKV), 0.02),
        'k_a': 1.0 + nrm((L, D_RWKV), 0.02),
        'r_k': nrm((L, H_RWKV, HEAD_DIM), 0.1),
        'lnx_g': 1.0 + nrm((L, D_RWKV), 0.02),
        'lnx_b': nrm((L, D_RWKV), 0.02),
        'w_out': nrm((L, d_mix, D_MODEL), DN_BETA * d_mix ** -0.5),
        'ln1_g': 1.0 + nrm((L, D_MODEL), 0.02),
        'ln1_b': nrm((L, D_MODEL), 0.02),
        'ln2_g': 1.0 + nrm((L, D_MODEL), 0.02),
        'ln2_b': nrm((L, D_MODEL), 0.02),
        'peer_wq': nrm((L, D_MODEL, PEER_HEADS * PEER_DK), D_MODEL ** -0.5),
        'peer_k1': nrm((L, N_KEYS, PEER_DK // 2), (PEER_DK // 2) ** -0.5),
        'peer_k2': nrm((L, N_KEYS, PEER_DK // 2), (PEER_DK // 2) ** -0.5),
        'peer_u': nrm((L, N_EXPERTS, D_MODEL), D_MODEL ** -0.5),
        'peer_v': nrm((L, N_EXPERTS, D_MODEL), DN_BETA * PEER_HEADS ** -0.5),
    }


def reference(x_prompt, x_sample, cache_k, cache_v, state_wkv, state_shift, ln_emb_g, ln_emb_b,
              w_in, mu_rkv, mu_x, w0, w1, w2, a0, a1, a2, g1, g2, k_k, k_a, r_k, lnx_g, lnx_b,
              w_out, ln1_g, ln1_b, ln2_g, ln2_b, peer_wq, peer_k1, peer_k2, peer_u, peer_v):
    xp = layer_norm(x_prompt, ln_emb_g, ln_emb_b)
    xs = layer_norm(x_sample, ln_emb_g, ln_emb_b)
    Bp = xp.shape[0]
    keep = min(WIN_MAX, xp.shape[1])
    ck_p, cv_p, wkv_p, sh_p = [], [], [], []
    ck_s, cv_s, wkv_s, sh_s = [], [], [], []
    for l in range(DEPTH):
        rw = (w_in[l], mu_rkv[l], mu_x[l], w0[l], w1[l], w2[l], a0[l], a1[l], a2[l], g1[l], g2[l],
              k_k[l], k_a[l], r_k[l], lnx_g[l], lnx_b[l])
        pm = (w_out[l], ln1_g[l], ln1_b[l], ln2_g[l], ln2_b[l],
              peer_wq[l], peer_k1[l], peer_k2[l], peer_u[l], peer_v[l])
        yr, wkv, q, k, v = rwkv7_and_qkv(xp, jnp.zeros_like(xp[:, 0]),
                                         jnp.zeros((Bp, H_RWKV, HEAD_DIM, HEAD_DIM), jnp.float32), *rw)
        ya = dilated_attention_prompt(q, k, v)
        ck_p.append(k[:, -keep:])
        cv_p.append(v[:, -keep:])
        wkv_p.append(wkv.astype(state_wkv.dtype))
        sh_p.append(xp[:, -1])
        xp = post_mix(xp, yr, ya, *pm)
        yr, wkv, q, k, v = rwkv7_and_qkv(xs, state_shift[l], state_wkv[l], *rw)
        ya = dilated_attention_sample(q, k, v, cache_k[l], cache_v[l])
        ck_s.append(k)
        cv_s.append(v)
        wkv_s.append(wkv.astype(state_wkv.dtype))
        sh_s.append(xs[:, -1])
        xs = post_mix(xs, yr, ya, *pm)
    return (xp, xs, jnp.stack(ck_p, 0), jnp.stack(cv_p, 0), jnp.stack(wkv_p, 0), jnp.stack(sh_p, 0),
            jnp.stack(ck_s, 0), jnp.stack(cv_s, 0), jnp.stack(wkv_s, 0), jnp.stack(sh_s, 0))
```

```python
import functools
import math

import jax
import jax.numpy as jnp
from jax import lax
from jax.experimental import pallas as pl
from jax.experimental.pallas import tpu as pltpu

F32 = jnp.float32
BF16 = jnp.bfloat16

HEAD_DIM = 64
LANES = 128
LN_EPS = 1e-5
GN_EPS = 64e-5
BRANCHES = ((128, 1), (512, 4), (2048, 16))
ATT_BLOCK = 128
PEER_HEADS = 8
PEER_TOPK = 16
N_KEYS = 128
VMEM_LIMIT = 56 * 1024 * 1024


def _cparams(*sem):
    return pltpu.CompilerParams(dimension_semantics=sem, vmem_limit_bytes=VMEM_LIMIT)


def _layer_norm(x, g, b):
    mu = jnp.mean(x, axis=-1, keepdims=True)
    xc = x - mu
    var = jnp.mean(xc * xc, axis=-1, keepdims=True)
    return xc * lax.rsqrt(var + LN_EPS) * g + b


def _softplus(x):
    return jnp.maximum(x, 0.0) + jnp.log(1.0 + jnp.exp(-jnp.abs(x)))


def _sigmoid(x):
    return 1.0 / (1.0 + jnp.exp(-x))


def _bf16_parts(x, n):
    parts, rem = [], x
    for _ in range(n):
        p = rem.astype(BF16)
        parts.append(p)
        rem = rem - p.astype(F32)
    return parts


def _prec(a):
    return lax.Precision.HIGHEST if a.dtype == F32 else None


def _dot(a, b):
    return jnp.dot(a, b, preferred_element_type=F32, precision=_prec(a))


def _dot_nt(a, b):
    return lax.dot_general(a, b, (((1,), (1,)), ((), ())), preferred_element_type=F32, precision=_prec(a))


def _dot_tn(a, b):
    return lax.dot_general(a, b, (((0,), (0,)), ((), ())), preferred_element_type=F32, precision=_prec(a))


def _ln_lora_kernel(*refs, seq_len, apply_ln, has_init):
    if has_init:
        x_ref, init_ref = refs[0], refs[1]
        refs = refs[2:]
    else:
        x_ref, init_ref = refs[0], None
        refs = refs[1:]
    (g_ref, b_ref, mu_ref, w1_ref, w2_ref, w0_ref, a1_ref, a2_ref, a0_ref, g1_ref, g2_ref,
     xn_ref, xnb_ref, lw_ref, a_ref, gate_ref, carry_ref) = refs
    i = pl.program_id(0)
    tm = x_ref.shape[0]

    @pl.when(i == 0)
    def _():
        carry_ref[...] = jnp.zeros_like(carry_ref)

    x = x_ref[...]
    xn = _layer_norm(x, g_ref[...], b_ref[...]) if apply_ln else x
    row = lax.broadcasted_iota(jnp.int32, (tm, 1), 0)
    rolled = pltpu.roll(xn, 1, 0)
    xprev = jnp.where(row == 0, carry_ref[7:8, :], rolled)
    grow = i * tm + row
    if seq_len & (seq_len - 1) == 0:
        is_start = (grow & (seq_len - 1)) == 0
    else:
        is_start = lax.rem(grow, seq_len) == 0
    init = init_ref[...] if has_init else 0.0
    xprev = jnp.where(is_start, init, xprev)
    carry_ref[...] = xn[tm - 8:tm, :]

    xx = xprev - xn
    xw = (xn + xx * mu_ref[0:1, :]).astype(BF16)
    xa = (xn + xx * mu_ref[1:2, :]).astype(BF16)
    xg = (xn + xx * mu_ref[2:3, :]).astype(BF16)
    wpre = w0_ref[...] + _dot(jnp.tanh(_dot(xw, w1_ref[...])).astype(BF16), w2_ref[...])
    apre = a0_ref[...] + _dot(_dot(xa, a1_ref[...]).astype(BF16), a2_ref[...])
    gate = _dot(_sigmoid(_dot(xg, g1_ref[...])).astype(BF16), g2_ref[...])
    lw_ref[...] = -jnp.exp(-_softplus(-wpre) - 0.5)
    a_ref[...] = _sigmoid(apre)
    gate_ref[...] = gate
    xn_ref[...] = xn
    xnb_ref[...] = xn.astype(BF16)


def _ln_lora(x, init, seq_len, apply_ln, ln_g, ln_b, p, tm):
    n, d = x.shape
    dr = p["w2"].shape[1]
    has_init = init is not None
    row_spec = pl.BlockSpec((tm, d), lambda i: (i, 0))
    out_spec = pl.BlockSpec((tm, dr), lambda i: (i, 0))

    def full(a):
        return pl.BlockSpec(a.shape, lambda i: (0,) * a.ndim)

    consts = [ln_g, ln_b, p["mu_x"], p["w1"], p["w2"], p["w0"], p["a1"], p["a2"], p["a0"], p["g1"], p["g2"]]
    ins = [x] + ([init] if has_init else []) + consts
    in_specs = [row_spec] + ([row_spec] if has_init else []) + [full(a) for a in consts]
    return pl.pallas_call(
        functools.partial(_ln_lora_kernel, seq_len=seq_len, apply_ln=apply_ln, has_init=has_init),
        grid=(n // tm,),
        in_specs=in_specs,
        out_specs=[row_spec, row_spec, out_spec, out_spec, out_spec],
        out_shape=[jax.ShapeDtypeStruct((n, d), F32), jax.ShapeDtypeStruct((n, d), BF16),
                   jax.ShapeDtypeStruct((n, dr), F32), jax.ShapeDtypeStruct((n, dr), F32),
                   jax.ShapeDtypeStruct((n, dr), F32)],
        scratch_shapes=[pltpu.VMEM((8, d), F32)],
        compiler_params=_cparams("arbitrary"),
        name="ln_lora",
    )(*ins)


def _mm_kernel(a_ref, b_ref, o_ref):
    o_ref[...] = _dot(a_ref[...], b_ref[...])


def _matmul(a, b, tm, tn):
    m, k = a.shape
    n = b.shape[1]
    return pl.pallas_call(
        _mm_kernel,
        grid=(m // tm, n // tn),
        in_specs=[pl.BlockSpec((tm, k), lambda i, j: (i, 0)), pl.BlockSpec((k, tn), lambda i, j: (0, j))],
        out_specs=pl.BlockSpec((tm, tn), lambda i, j: (i, j)),
        out_shape=jax.ShapeDtypeStruct((m, n), F32),
        compiler_params=_cparams("parallel", "parallel"),
        name="matmul",
    )(a, b)


def _wkv_kernel(r_ref, k_ref, v_ref, lw_ref, a_ref, g_ref, prev_ref, mu_ref, par_ref, s0_ref,
                y_ref, sout_ref, s_ref, carry_ref, *, chunk, mm_dtype):
    L = chunk
    c = pl.program_id(2)

    @pl.when(c == 0)
    def _():
        s_ref[...] = s0_ref[...]
        carry_ref[0:3, :] = prev_ref[...]

    lane = lax.broadcasted_iota(jnp.int32, (1, LANES), 1)
    head_a = lane < HEAD_DIM
    row = lax.broadcasted_iota(jnp.int32, (L, 1), 0)

    def shifted(x0, j):
        return jnp.where(row == 0, carry_ref[j:j + 1, :], pltpu.roll(x0, 1, 0))

    r0, k0, v0 = r_ref[...], k_ref[...], v_ref[...]
    rs, ks, vs = shifted(r0, 0), shifted(k0, 1), shifted(v0, 2)
    carry_ref[0:1, :] = r0[L - 1:L, :]
    carry_ref[1:2, :] = k0[L - 1:L, :]
    carry_ref[2:3, :] = v0[L - 1:L, :]
    r = r0 + (rs - r0) * mu_ref[0:1, :]
    k = k0 + (ks - k0) * mu_ref[1:2, :]
    v = v0 + (vs - v0) * mu_ref[2:3, :]
    lw = lw_ref[...]
    a_lr = a_ref[...]
    k_k, k_a, r_k = par_ref[0:1, :], par_ref[1:2, :], par_ref[2:3, :]
    lnx_g, lnx_b = par_ref[3:4, :], par_ref[4:5, :]

    gi = lax.broadcasted_iota(jnp.int32, (LANES, LANES), 0) // HEAD_DIM
    gj = lax.broadcasted_iota(jnp.int32, (LANES, LANES), 1) // HEAD_DIM
    seg = jnp.where(gi == gj, 1.0, 0.0).astype(mm_dtype)

    def head_sum(x):
        if mm_dtype == F32:
            return _dot(x, seg)
        return sum(_dot(p, seg) for p in _bf16_parts(x, 3))

    cum = lw
    step = 1
    while step < L:
        cum = cum + jnp.where(row >= step, pltpu.roll(cum, step, 0), 0.0)
        step *= 2
    g_in = jnp.exp(cum)
    g_ex = jnp.exp(cum - lw)
    g_inv = jnp.exp(-cum)

    kk = k * k_k
    kk = kk / jnp.maximum(jnp.sqrt(head_sum(kk * kk)), 1e-12)
    kmod = k * (1.0 + (a_lr - 1.0) * k_a)
    at = (-kk) * g_ex
    rt = r * g_in
    bt = kk * a_lr * g_inv
    kt = kmod * g_inv

    def pair(x):
        return jnp.concatenate([jnp.where(head_a, x, 0.0), jnp.where(head_a, 0.0, x)], axis=0)

    md = mm_dtype
    xa, xr, xb, xk, vp = (pair(t).astype(md) for t in (at, rt, bt, kt, v))
    s0 = s_ref[...]
    s0m = s0.astype(md)
    pi = lax.broadcasted_iota(jnp.int32, (2 * L, 2 * L), 0)
    pj = lax.broadcasted_iota(jnp.int32, (2 * L, 2 * L), 1)
    strict, incl = pi > pj, pi >= pj
    n_ab = jnp.where(strict, _dot_nt(xa, xb), 0.0)
    n_ak = jnp.where(strict, _dot_nt(xa, xk), 0.0)
    p_rb = jnp.where(incl, _dot_nt(xr, xb), 0.0)
    p_rk = jnp.where(incl, _dot_nt(xr, xk), 0.0)
    u = _dot_nt(xa, s0m) + _dot(n_ak.astype(md), vp)
    levels = int(round(math.log2(L)))
    npow = n_ab
    for lvl in range(levels):
        nb = npow.astype(md)
        u = u + _dot(nb, u.astype(md))
        if lvl + 1 < levels:
            npow = _dot(nb, nb)
    um = u.astype(md)
    y_pair = _dot_nt(xr, s0m) + _dot(p_rb.astype(md), um) + _dot(p_rk.astype(md), vp)
    y = y_pair[0:L, :] + y_pair[L:2 * L, :]
    s_new = (s0 + _dot_tn(um, xb) + _dot_tn(vp, xk)) * g_in[L - 1:L, :]
    s_ref[...] = s_new
    sout_ref[...] = s_new

    inv_n = 1.0 / HEAD_DIM
    yc = y - head_sum(y) * inv_n
    var = head_sum(yc * yc) * inv_n
    yn = yc * lax.rsqrt(var + GN_EPS) * lnx_g + lnx_b
    yn = yn + head_sum(r * kmod * r_k) * v
    y_ref[...] = yn * g_ref[...]


def _wkv(proj, lw, a_lr, gate, prev, mu_rkv, par, s0, n_seq, seq_len, chunk):
    rows = n_seq * seq_len
    dr = lw.shape[1]
    nhp = dr // LANES
    nc = seq_len // chunk

    def col(off):
        return pl.BlockSpec((chunk, LANES), lambda b, h, c: (b * nc + c, off * nhp + h))

    return pl.pallas_call(
        functools.partial(_wkv_kernel, chunk=chunk, mm_dtype=BF16 if chunk >= 16 else F32),
        grid=(n_seq, nhp, nc),
        in_specs=[col(0), col(1), col(2), col(0), col(0), col(0),
                  pl.BlockSpec((None, 3, LANES), lambda b, h, c: (b, 0, h)),
                  pl.BlockSpec((3, LANES), lambda b, h, c: (0, h)),
                  pl.BlockSpec((5, LANES), lambda b, h, c: (0, h)),
                  pl.BlockSpec((None, None, LANES, LANES), lambda b, h, c: (b, h, 0, 0))],
        out_specs=[col(0), pl.BlockSpec((None, None, LANES, LANES), lambda b, h, c: (b, h, 0, 0))],
        out_shape=[jax.ShapeDtypeStruct((rows, dr), F32),
                   jax.ShapeDtypeStruct((n_seq, nhp, LANES, LANES), F32)],
        scratch_shapes=[pltpu.VMEM((LANES, LANES), F32), pltpu.VMEM((8, LANES), F32)],
        compiler_params=_cparams("parallel", "parallel", "arbitrary"),
        name="wkv",
    )(proj, proj, proj, lw, a_lr, gate, prev, mu_rkv, par, s0)


def _pair_state(s):
    b, h, n, _ = s.shape
    s = s.reshape(b, h // 2, 2, n, n)
    z = jnp.zeros_like(s[:, :, 0])
    top = jnp.concatenate([s[:, :, 0], z], axis=-1)
    bot = jnp.concatenate([z, s[:, :, 1]], axis=-1)
    return jnp.concatenate([top, bot], axis=-2)


def _unpair_state(sp):
    n = HEAD_DIM
    b, hp = sp.shape[:2]
    return jnp.stack([sp[:, :, :n, :n], sp[:, :, n:, n:]], axis=2).reshape(b, hp * 2, n, n)


def _alibi_slope(h, n_heads):
    return 2.0 ** (-8.0 * (h + 1) / n_heads)


def _att_prompt_kernel(q_ref, kc_ref, vc_ref, kp_ref, vp_ref, o_ref, lse_ref, *, dil, span, n_heads):
    blk = ATT_BLOCK
    n = pl.program_id(1)
    scale = HEAD_DIM ** -0.5
    qi = lax.broadcasted_iota(jnp.int32, (2 * blk, 2 * blk), 0) & (blk - 1)
    ki = lax.broadcasted_iota(jnp.int32, (2 * blk, 2 * blk), 1)
    jd = blk + qi - ki
    valid = (jd >= 0) & (jd <= span) & ((n > 0) | (ki >= blk))
    dist = (jd * dil).astype(F32)
    top = lax.broadcasted_iota(jnp.int32, (2 * blk, 1), 0) < blk
    lane = lax.broadcasted_iota(jnp.int32, (1, LANES), 1)
    head_a = lane < HEAD_DIM
    for hp in range(n_heads // 2):
        sl = slice(hp * LANES, (hp + 1) * LANES)
        q = q_ref[:, sl]
        qs = jnp.concatenate([jnp.where(head_a, q, 0.0), jnp.where(head_a, 0.0, q)], axis=0).astype(BF16)
        kw = jnp.concatenate([kp_ref[:, sl], kc_ref[:, sl]], axis=0).astype(BF16)
        vw = jnp.concatenate([vp_ref[:, sl], vc_ref[:, sl]], axis=0).astype(BF16)
        slope = jnp.where(top, _alibi_slope(2 * hp, n_heads), _alibi_slope(2 * hp + 1, n_heads))
        s = _dot_nt(qs, kw) * scale - slope * dist
        s = jnp.where(valid, s, -jnp.inf)
        m = jnp.max(s, axis=1, keepdims=True)
        p = jnp.exp(s - m)
        l = jnp.sum(p, axis=1, keepdims=True)
        o2 = _dot(p.astype(BF16), vw) / l
        lse2 = m + jnp.log(l)
        o_ref[:, sl] = jnp.where(head_a, o2[0:blk, :], o2[blk:2 * blk, :])
        lse_ref[:, sl] = jnp.where(head_a, lse2[0:blk, :], lse2[blk:2 * blk, :])


def _att_prompt_branch(proj, seq, d_att, window, dil):
    blk = ATT_BLOCK
    a_len = seq // dil
    nb = a_len // blk
    view = proj.reshape(a_len, dil * proj.shape[1])
    ngrp = proj.shape[1] // d_att

    def cur(g):
        return pl.BlockSpec((blk, d_att), lambda r, n: (n, r * ngrp + g))

    def prev(g):
        return pl.BlockSpec((blk, d_att), lambda r, n: (jnp.maximum(n - 1, 0), r * ngrp + g))

    out_spec = pl.BlockSpec((blk, d_att), lambda r, n: (n, r))
    o, lse = pl.pallas_call(
        functools.partial(_att_prompt_kernel, dil=dil, span=window // dil, n_heads=d_att // HEAD_DIM),
        grid=(dil, nb),
        in_specs=[cur(3), cur(4), cur(5), prev(4), prev(5)],
        out_specs=[out_spec, out_spec],
        out_shape=[jax.ShapeDtypeStruct((a_len, dil * d_att), F32)] * 2,
        compiler_params=_cparams("parallel", "arbitrary"),
        name="att_prompt_d%d" % dil,
    )(view, view, view, view, view)
    return o.reshape(seq, d_att), lse.reshape(seq, d_att)


def _att_combine_kernel(o1, o2, o3, l1, l2, l3, y_ref):
    a, b, c = l1[...], l2[...], l3[...]
    m = jnp.maximum(jnp.maximum(a, b), c)
    wa, wb, wc = jnp.exp(a - m), jnp.exp(b - m), jnp.exp(c - m)
    y_ref[...] = (wa * o1[...] + wb * o2[...] + wc * o3[...]) / (wa + wb + wc)


def _att_combine(outs, lses, tm):
    n, d = outs[0].shape
    spec = pl.BlockSpec((tm, d), lambda i: (i, 0))
    return pl.pallas_call(
        _att_combine_kernel,
        grid=(n // tm,),
        in_specs=[spec] * 6,
        out_specs=spec,
        out_shape=jax.ShapeDtypeStruct((n, d), F32),
        compiler_params=_cparams("parallel"),
        name="att_combine",
    )(*outs, *lses)


def _att_sample_kernel(q_ref, kn_ref, vn_ref, ck_ref, cv_ref, y_ref, *, n_heads):
    t_new = q_ref.shape[0]
    wb = ck_ref.shape[0]
    scale = HEAD_DIM ** -0.5
    rows = 2 * t_new
    row = lax.broadcasted_iota(jnp.int32, (rows, 1), 0)
    t = jnp.where(row < t_new, row, row - t_new)
    delta_c = wb + t - lax.broadcasted_iota(jnp.int32, (rows, wb), 1)
    delta_n = t - lax.broadcasted_iota(jnp.int32, (rows, t_new), 1)
    dist_c, dist_n = delta_c.astype(F32), delta_n.astype(F32)
    lane = lax.broadcasted_iota(jnp.int32, (1, LANES), 1)
    head_a = lane < HEAD_DIM
    for hp in range(n_heads // 2):
        sl = slice(hp * LANES, (hp + 1) * LANES)
        q = q_ref[:, sl]
        qs = jnp.concatenate([jnp.where(head_a, q, 0.0), jnp.where(head_a, 0.0, q)], axis=0)
        kc = ck_ref[:, sl].astype(BF16)
        vc = cv_ref[:, sl].astype(BF16)
        kn, vn = kn_ref[:, sl], vn_ref[:, sl]
        slope = jnp.where(row < t_new, _alibi_slope(2 * hp, n_heads), _alibi_slope(2 * hp + 1, n_heads))
        sc = _dot_nt(qs.astype(BF16), kc) * scale - slope * dist_c
        sn = _dot_nt(qs, kn) * scale - slope * dist_n
        outs, lses = [], []
        for window, dil in BRANCHES:
            sh = int(round(math.log2(dil)))
            span = window // dil
            ok_c = ((delta_c & (dil - 1)) == 0) & ((delta_c >> sh) <= span)
            ok_n = (delta_n >= 0) & ((delta_n & (dil - 1)) == 0) & ((delta_n >> sh) <= span)
            zc = jnp.where(ok_c, sc, -jnp.inf)
            zn = jnp.where(ok_n, sn, -jnp.inf)
            m = jnp.maximum(jnp.max(zc, axis=1, keepdims=True), jnp.max(zn, axis=1, keepdims=True))
            pc, pn = jnp.exp(zc - m), jnp.exp(zn - m)
            l = jnp.sum(pc, axis=1, keepdims=True) + jnp.sum(pn, axis=1, keepdims=True)
            outs.append((_dot(pc.astype(BF16), vc) + _dot(pn, vn)) / l)
            lses.append(m + jnp.log(l))
        m = jnp.maximum(jnp.maximum(lses[0], lses[1]), lses[2])
        ws = [jnp.exp(x - m) for x in lses]
        y2 = (ws[0] * outs[0] + ws[1] * outs[1] + ws[2] * outs[2]) / (ws[0] + ws[1] + ws[2])
        y_ref[:, sl] = jnp.where(head_a, y2[0:t_new, :], y2[t_new:rows, :])


def _att_sample(proj, cache_k, cache_v, n_seq, t_new, d_att):
    wb = cache_k.shape[1]
    ck = cache_k.reshape(n_seq, wb, d_att)
    cv = cache_v.reshape(n_seq, wb, d_att)

    def new(g):
        return pl.BlockSpec((t_new, d_att), lambda b: (b, g))

    cache_spec = pl.BlockSpec((None, wb, d_att), lambda b: (b, 0, 0))
    return pl.pallas_call(
        functools.partial(_att_sample_kernel, n_heads=d_att // HEAD_DIM),
        grid=(n_seq,),
        in_specs=[new(3), new(4), new(5), cache_spec, cache_spec],
        out_specs=pl.BlockSpec((t_new, d_att), lambda b: (b, 0)),
        out_shape=jax.ShapeDtypeStruct((n_seq * t_new, d_att), F32),
        compiler_params=_cparams("parallel"),
        name="att_sample",
    )(proj, proj, proj, ck, cv)


def _outproj_kernel(ya_ref, yb_ref, xn_ref, w_ref, g_ref, b_ref, h_ref, hb_ref, *, alpha):
    half = ya_ref.shape[1]
    mix = _dot(ya_ref[...].astype(BF16), w_ref[0:half, :]) + _dot(yb_ref[...].astype(BF16), w_ref[half:, :])
    h = _layer_norm(alpha * xn_ref[...] + mix, g_ref[...], b_ref[...])
    h_ref[...] = h
    hb_ref[...] = h.astype(BF16)


def _outproj(ya, yb, xn, w_out, g, b, alpha, tm):
    n, d = xn.shape
    row = pl.BlockSpec((tm, d), lambda i: (i, 0))
    half = pl.BlockSpec((tm, ya.shape[1]), lambda i: (i, 0))
    vec = pl.BlockSpec((1, d), lambda i: (0, 0))
    return pl.pallas_call(
        functools.partial(_outproj_kernel, alpha=alpha),
        grid=(n // tm,),
        in_specs=[half, half, row, pl.BlockSpec(w_out.shape, lambda i: (0, 0)), vec, vec],
        out_specs=[row, row],
        out_shape=[jax.ShapeDtypeStruct((n, d), F32), jax.ShapeDtypeStruct((n, d), BF16)],
        compiler_params=_cparams("parallel"),
        name="outproj_ln",
    )(ya, yb, xn, w_out, g, b)


def _top16(s, want_rank):
    n, t = s.shape
    idx = lax.broadcasted_iota(jnp.int32, (n, t), 0)
    slot = lax.broadcasted_iota(jnp.int32, (PEER_TOPK, t), 0)
    rank = jnp.full((n, t), 99.0, F32)
    vals = jnp.zeros((PEER_TOPK, t), F32)
    picks = jnp.zeros((PEER_TOPK, t), jnp.int32)
    for it in range(PEER_TOPK):
        m = jnp.max(s, axis=0, keepdims=True)
        pick = jnp.min(jnp.where(s == m, idx, n), axis=0, keepdims=True)
        hit = idx == pick
        s = jnp.where(hit, -jnp.inf, s)
        vals = jnp.where(slot == it, m, vals)
        if want_rank:
            rank = jnp.where(hit, float(it), rank)
        else:
            picks = jnp.where(slot == it, pick, picks)
    return vals, (rank if want_rank else picks)


def _peer_select_kernel(q_ref, k1_ref, k2_ref, rank2_ref, e2_ref, len1_ref, e1z_ref):
    half = k1_ref.shape[1]
    tt = q_ref.shape[0]
    qb = q_ref[...].astype(BF16)
    s1_all = _dot_nt(k1_ref[...], qb[:, 0:half])
    s2_all = _dot_nt(k2_ref[...], qb[:, half:2 * half])
    slot = lax.broadcasted_iota(jnp.int32, (PEER_TOPK, LANES), 0)
    for c in range(tt // LANES):
        cs = slice(c * LANES, (c + 1) * LANES)
        s1, s2 = s1_all[:, cs], s2_all[:, cs]
        v1, rank1 = _top16(s1, True)
        v2, rank2 = _top16(s2, True)
        cand = jnp.concatenate([v1[i:i + 1, :] + v2 for i in range(PEER_TOPK)], axis=0)
        top_s, picks = _top16(cand, False)
        prow = picks >> 4
        mx = v1[0:1, :] + v2[0:1, :]
        z = jnp.sum(jnp.exp(top_s - mx), axis=0, keepdims=True)
        len1 = jnp.zeros(s1.shape, F32)
        for i in range(PEER_TOPK):
            len_i = jnp.sum(jnp.where(prow == i, 1.0, 0.0), axis=0, keepdims=True)
            len1 = jnp.where(rank1 == float(i), len_i, len1)
        rank2_ref[:, cs] = rank2
        len1_ref[:, cs] = len1
        e2_ref[:, cs] = jnp.exp(s2 - v2[0:1, :])
        e1z_ref[:, cs] = jnp.exp(s1 - v1[0:1, :]) / z


def _peer_select(q, k1, k2, tt):
    n = q.shape[0]
    dk = 2 * k1.shape[1]
    nh = q.shape[1] // dk
    out_spec = pl.BlockSpec((None, N_KEYS, tt), lambda i, h: (h, 0, i))
    shape = jax.ShapeDtypeStruct((nh, N_KEYS, n), F32)
    return pl.pallas_call(
        _peer_select_kernel,
        grid=(n // tt, nh),
        in_specs=[pl.BlockSpec((tt, dk), lambda i, h: (i, h)),
                  pl.BlockSpec(k1.shape, lambda i, h: (0, 0)), pl.BlockSpec(k2.shape, lambda i, h: (0, 0))],
        out_specs=[out_spec] * 4,
        out_shape=[shape] * 4,
        compiler_params=_cparams("parallel", "parallel"),
        name="peer_select",
    )(q, k1, k2)


def _gelu(x):
    return 0.5 * x * (1.0 + lax.erf(x * (2.0 ** -0.5)))


def _peer_dense_kernel(hb_ref, h_ref, rank2_ref, e2_ref, len1_ref, e1z_ref, u_ref, vt_ref, g_ref, b_ref,
                       y_ref, acc_ref, act_ref, *, alpha):
    e = pl.program_id(1)
    eb = u_ref.shape[0]
    nh = rank2_ref.shape[0]

    @pl.when(e == 0)
    def _():
        acc_ref[...] = jnp.zeros_like(acc_ref)

    ht = _dot_nt(u_ref[...], hb_ref[...])
    for ii in range(eb // N_KEYS):
        i1 = e * (eb // N_KEYS) + ii
        gate = jnp.zeros((N_KEYS, ht.shape[1]), F32)
        for hh in range(nh):
            keep = rank2_ref[hh] < len1_ref[hh, pl.ds(i1, 1), :]
            gate = gate + jnp.where(keep, e2_ref[hh] * e1z_ref[hh, pl.ds(i1, 1), :], 0.0)
        rows = slice(ii * N_KEYS, (ii + 1) * N_KEYS)
        act_ref[rows, :] = (_gelu(ht[rows, :]) * gate).astype(BF16)
    acc_ref[...] += _dot(vt_ref[...], act_ref[...])

    @pl.when(e == pl.num_programs(1) - 1)
    def _():
        z = alpha * h_ref[...] + acc_ref[...].T
        y_ref[...] = _layer_norm(z, g_ref[...], b_ref[...])


def _peer_dense(hb, h, sel, u_b, vt_b, g, b, alpha, tt, eb):
    n, d = h.shape
    n_exp = u_b.shape[0]
    nh = sel[0].shape[0]
    row = pl.BlockSpec((tt, d), lambda i, e: (i, 0))
    sel_spec = pl.BlockSpec((nh, N_KEYS, tt), lambda i, e: (0, 0, i))
    vec = pl.BlockSpec((1, d), lambda i, e: (0, 0))
    return pl.pallas_call(
        functools.partial(_peer_dense_kernel, alpha=alpha),
        grid=(n // tt, n_exp // eb),
        in_specs=[row, row, sel_spec, sel_spec, sel_spec, sel_spec,
                  pl.BlockSpec((eb, d), lambda i, e: (e, 0)), pl.BlockSpec((d, eb), lambda i, e: (0, e)),
                  vec, vec],
        out_specs=row,
        out_shape=jax.ShapeDtypeStruct((n, d), F32),
        scratch_shapes=[pltpu.VMEM((d, tt), F32), pltpu.VMEM((eb, tt), BF16)],
        compiler_params=_cparams("parallel", "arbitrary"),
        name="peer_dense",
    )(hb, h, *sel, u_b, vt_b, g, b)


def _tile(n, pref):
    t = pref
    while n % t:
        t //= 2
    return t


def _post_mix(xn, y_rwkv, y_att, lp, alpha):
    n = xn.shape[0]
    h, hb = _outproj(y_rwkv, y_att, xn, lp["w_out"], lp["ln1_g"], lp["ln1_b"], alpha, _tile(n, 512))
    q = _matmul(hb, lp["peer_wq"], _tile(n, 1024), 1024)
    sel = _peer_select(q, lp["peer_k1"], lp["peer_k2"], _tile(n, 256))
    return _peer_dense(hb, h, sel, lp["peer_u"], lp["peer_vt"], lp["ln2_g"], lp["ln2_b"], alpha,
                       _tile(n, 512), 512)


def kernel(x_prompt, x_sample, cache_k, cache_v, state_wkv, state_shift, ln_emb_g, ln_emb_b, w_in, mu_rkv, mu_x, w0, w1, w2, a0, a1, a2, g1, g2, k_k, k_a, r_k, lnx_g, lnx_b, w_out, ln1_g, ln1_b, ln2_g, ln2_b, peer_wq, peer_k1, peer_k2, peer_u, peer_v):
    depth = w_in.shape[0]
    bp, seq, d = x_prompt.shape
    bs, t_new, _ = x_sample.shape
    dr = w2.shape[-1]
    d_att = d - dr
    h_att = d_att // HEAD_DIM
    h_rwkv = dr // HEAD_DIM
    alpha = (2.0 * depth) ** 0.25
    keep = min(max(w for w, _ in BRANCHES), seq)
    row2 = lambda a: a.reshape(1, -1)

    xp = x_prompt.reshape(bp * seq, d)
    xs = x_sample.reshape(bs * t_new, d)
    emb_g, emb_b = row2(ln_emb_g), row2(ln_emb_b)
    outs = {k: [] for k in ("ck_p", "cv_p", "wkv_p", "sh_p", "ck_s", "cv_s", "wkv_s", "sh_s")}
    for l in range(depth):
        lp = {
            "mu_x": mu_x[l], "w1": w1[l].astype(BF16), "w2": w2[l].astype(BF16), "w0": row2(w0[l]),
            "a1": a1[l].astype(BF16), "a2": a2[l].astype(BF16), "a0": row2(a0[l]),
            "g1": g1[l].astype(BF16), "g2": g2[l].astype(BF16),
            "w_out": w_out[l].astype(BF16), "ln1_g": row2(ln1_g[l]), "ln1_b": row2(ln1_b[l]),
            "ln2_g": row2(ln2_g[l]), "ln2_b": row2(ln2_b[l]),
            "peer_wq": peer_wq[l].astype(BF16), "peer_k1": peer_k1[l].astype(BF16),
            "peer_k2": peer_k2[l].astype(BF16), "peer_u": peer_u[l].astype(BF16),
            "peer_vt": peer_v[l].astype(BF16).T,
        }
        w_in_b = w_in[l].astype(BF16)
        mu3 = mu_rkv[l].reshape(3, dr)
        par = jnp.stack([k_k[l], k_a[l], r_k[l].reshape(-1), lnx_g[l], lnx_b[l]], axis=0)
        first = l == 0

        xn, xnb, lw, a_lr, gate = _ln_lora(xp, None, seq, first, emb_g, emb_b, lp, _tile(bp * seq, 512))
        proj = _matmul(xnb, w_in_b, _tile(bp * seq, 1024), 1536)
        y_rwkv, s_pair = _wkv(proj, lw, a_lr, gate, jnp.zeros((bp, 3, dr), F32), mu3, par,
                              jnp.zeros((bp, h_rwkv // 2, LANES, LANES), F32), bp, seq, 64)
        att = []
        for b in range(bp):
            pb = proj[b * seq:(b + 1) * seq] if bp > 1 else proj
            res = [_att_prompt_branch(pb, seq, d_att, w, dl) for w, dl in BRANCHES]
            att.append(_att_combine([o for o, _ in res], [s for _, s in res], _tile(seq, 512)))
        y_att = att[0] if bp == 1 else jnp.concatenate(att, axis=0)
        proj3 = proj.reshape(bp, seq, -1)
        outs["ck_p"].append(proj3[:, seq - keep:, 4 * dr:5 * dr].reshape(bp, keep, h_att, HEAD_DIM))
        outs["cv_p"].append(proj3[:, seq - keep:, 5 * dr:6 * dr].reshape(bp, keep, h_att, HEAD_DIM))
        outs["wkv_p"].append(_unpair_state(s_pair).astype(state_wkv.dtype))
        outs["sh_p"].append(xn.reshape(bp, seq, d)[:, -1])
        xp = _post_mix(xn, y_rwkv, y_att, lp, alpha)

        init = jnp.repeat(state_shift[l], t_new, axis=0)
        xn, xnb, lw, a_lr, gate = _ln_lora(xs, init, t_new, first, emb_g, emb_b, lp, _tile(bs * t_new, 256))
        proj = _matmul(xnb, w_in_b, _tile(bs * t_new, 256), 1536)
        prev = _matmul(state_shift[l].astype(BF16), w_in_b[:, :3 * dr], bs, 3 * dr // 2)
        y_rwkv, s_pair = _wkv(proj, lw, a_lr, gate, prev.reshape(bs, 3, dr), mu3, par,
                              _pair_state(state_wkv[l].astype(F32)), bs, t_new, t_new)
        y_att = _att_sample(proj, cache_k[l], cache_v[l], bs, t_new, d_att)
        proj3 = proj.reshape(bs, t_new, -1)
        outs["ck_s"].append(proj3[:, :, 4 * dr:5 * dr].reshape(bs, t_new, h_att, HEAD_DIM))
        outs["cv_s"].append(proj3[:, :, 5 * dr:6 * dr].reshape(bs, t_new, h_att, HEAD_DIM))
        outs["wkv_s"].append(_unpair_state(s_pair).astype(state_wkv.dtype))
        outs["sh_s"].append(xn.reshape(bs, t_new, d)[:, -1])
        xs = _post_mix(xn, y_rwkv, y_att, lp, alpha)

    st = lambda k: jnp.stack(outs[k], 0)
    return (xp.reshape(bp, seq, d), xs.reshape(bs, t_new, d), st("ck_p"), st("cv_p"), st("wkv_p"), st("sh_p"),
            st("ck_s"), st("cv_s"), st("wkv_s"), st("sh_s"))
```

```python
import functools
import math

import jax
import jax.numpy as jnp
from jax import lax
from jax.experimental import pallas as pl
from jax.experimental.pallas import tpu as pltpu

F32 = jnp.float32
BF16 = jnp.bfloat16

HEAD_DIM = 64
LANES = 128
LN_EPS = 1e-5
GN_EPS = 64e-5
BRANCHES = ((128, 1), (512, 4), (2048, 16))
ATT_BLOCK = 128
PEER_HEADS = 8
PEER_TOPK = 16
N_KEYS = 128
VMEM_LIMIT = 56 * 1024 * 1024


def _cparams(*sem):
    return pltpu.CompilerParams(dimension_semantics=sem, vmem_limit_bytes=VMEM_LIMIT)


def _layer_norm(x, g, b):
    mu = jnp.mean(x, axis=-1, keepdims=True)
    xc = x - mu
    var = jnp.mean(xc * xc, axis=-1, keepdims=True)
    return xc * lax.rsqrt(var + LN_EPS) * g + b


def _softplus(x):
    return jnp.maximum(x, 0.0) + jnp.log(1.0 + jnp.exp(-jnp.abs(x)))


def _sigmoid(x):
    return 1.0 / (1.0 + jnp.exp(-x))


def _bf16_parts(x, n):
    parts, rem = [], x
    for _ in range(n):
        p = rem.astype(BF16)
        parts.append(p)
        rem = rem - p.astype(F32)
    return parts


def _prec(a):
    return lax.Precision.HIGHEST if a.dtype == F32 else None


def _dot(a, b):
    return jnp.dot(a, b, preferred_element_type=F32, precision=_prec(a))


def _dot_nt(a, b):
    return lax.dot_general(a, b, (((1,), (1,)), ((), ())), preferred_element_type=F32, precision=_prec(a))


def _dot_tn(a, b):
    return lax.dot_general(a, b, (((0,), (0,)), ((), ())), preferred_element_type=F32, precision=_prec(a))


def _ln_lora_kernel(*refs, seq_len, apply_ln, has_init):
    if has_init:
        x_ref, init_ref = refs[0], refs[1]
        refs = refs[2:]
    else:
        x_ref, init_ref = refs[0], None
        refs = refs[1:]
    (g_ref, b_ref, mu_ref, w1_ref, w2_ref, w0_ref, a1_ref, a2_ref, a0_ref, g1_ref, g2_ref,
     xn_ref, xnb_ref, lw_ref, a_ref, gate_ref, carry_ref) = refs
    i = pl.program_id(0)
    tm = x_ref.shape[0]

    @pl.when(i == 0)
    def _():
        carry_ref[...] = jnp.zeros_like(carry_ref)

    x = x_ref[...]
    xn = _layer_norm(x, g_ref[...], b_ref[...]) if apply_ln else x
    row = lax.broadcasted_iota(jnp.int32, (tm, 1), 0)
    rolled = pltpu.roll(xn, 1, 0)
    xprev = jnp.where(row == 0, carry_ref[7:8, :], rolled)
    grow = i * tm + row
    if seq_len & (seq_len - 1) == 0:
        is_start = (grow & (seq_len - 1)) == 0
    else:
        is_start = lax.rem(grow, seq_len) == 0
    init = init_ref[...] if has_init else 0.0
    xprev = jnp.where(is_start, init, xprev)
    carry_ref[...] = xn[tm - 8:tm, :]

    xx = xprev - xn
    xw = (xn + xx * mu_ref[0:1, :]).astype(BF16)
    xa = (xn + xx * mu_ref[1:2, :]).astype(BF16)
    xg = (xn + xx * mu_ref[2:3, :]).astype(BF16)
    wpre = w0_ref[...] + _dot(jnp.tanh(_dot(xw, w1_ref[...])).astype(BF16), w2_ref[...])
    apre = a0_ref[...] + _dot(_dot(xa, a1_ref[...]).astype(BF16), a2_ref[...])
    gate = _dot(_sigmoid(_dot(xg, g1_ref[...])).astype(BF16), g2_ref[...])
    lw_ref[...] = -jnp.exp(-_softplus(-wpre) - 0.5)
    a_ref[...] = _sigmoid(apre)
    gate_ref[...] = gate
    xn_ref[...] = xn
    xnb_ref[...] = xn.astype(BF16)


def _ln_lora(x, init, seq_len, apply_ln, ln_g, ln_b, p, tm):
    n, d = x.shape
    dr = p["w2"].shape[1]
    has_init = init is not None
    row_spec = pl.BlockSpec((tm, d), lambda i: (i, 0))
    out_spec = pl.BlockSpec((tm, dr), lambda i: (i, 0))

    def full(a):
        return pl.BlockSpec(a.shape, lambda i: (0,) * a.ndim)

    consts = [ln_g, ln_b, p["mu_x"], p["w1"], p["w2"], p["w0"], p["a1"], p["a2"], p["a0"], p["g1"], p["g2"]]
    ins = [x] + ([init] if has_init else []) + consts
    in_specs = [row_spec] + ([row_spec] if has_init else []) + [full(a) for a in consts]
    return pl.pallas_call(
        functools.partial(_ln_lora_kernel, seq_len=seq_len, apply_ln=apply_ln, has_init=has_init),
        grid=(n // tm,),
        in_specs=in_specs,
        out_specs=[row_spec, row_spec, out_spec, out_spec, out_spec],
        out_shape=[jax.ShapeDtypeStruct((n, d), F32), jax.ShapeDtypeStruct((n, d), BF16),
                   jax.ShapeDtypeStruct((n, dr), F32), jax.ShapeDtypeStruct((n, dr), F32),
                   jax.ShapeDtypeStruct((n, dr), F32)],
        scratch_shapes=[pltpu.VMEM((8, d), F32)],
        compiler_params=_cparams("arbitrary"),
        name="ln_lora",
    )(*ins)


def _mm_kernel(a_ref, b_ref, o_ref):
    o_ref[...] = _dot(a_ref[...], b_ref[...])


def _matmul(a, b, tm, tn):
    m, k = a.shape
    n = b.shape[1]
    return pl.pallas_call(
        _mm_kernel,
        grid=(m // tm, n // tn),
        in_specs=[pl.BlockSpec((tm, k), lambda i, j: (i, 0)), pl.BlockSpec((k, tn), lambda i, j: (0, j))],
        out_specs=pl.BlockSpec((tm, tn), lambda i, j: (i, j)),
        out_shape=jax.ShapeDtypeStruct((m, n), F32),
        compiler_params=_cparams("parallel", "parallel"),
        name="matmul",
    )(a, b)


def _wkv_kernel(r_ref, k_ref, v_ref, lw_ref, a_ref, g_ref, prev_ref, mu_ref, par_ref, s0_ref,
                y_ref, sout_ref, s_ref, carry_ref, *, chunk, mm_dtype):
    L = chunk
    c = pl.program_id(1)
    nhp = s_ref.shape[0]

    @pl.when(c == 0)
    def _():
        s_ref[...] = s0_ref[...]
        carry_ref[0:3, :] = prev_ref[...]

    lane = lax.broadcasted_iota(jnp.int32, (1, LANES), 1)
    head_a = lane < HEAD_DIM
    row = lax.broadcasted_iota(jnp.int32, (L, 1), 0)
    md = mm_dtype

    gi = lax.broadcasted_iota(jnp.int32, (LANES, LANES), 0) // HEAD_DIM
    gj = lax.broadcasted_iota(jnp.int32, (LANES, LANES), 1) // HEAD_DIM
    seg = jnp.where(gi == gj, 1.0, 0.0).astype(md)

    def head_sum(x):
        if md == F32:
            return _dot(x, seg)
        return sum(_dot(p, seg) for p in _bf16_parts(x, 3))

    def pair(x):
        return jnp.concatenate([jnp.where(head_a, x, 0.0), jnp.where(head_a, 0.0, x)], axis=0)

    pi = lax.broadcasted_iota(jnp.int32, (2 * L, 2 * L), 0)
    pj = lax.broadcasted_iota(jnp.int32, (2 * L, 2 * L), 1)
    strict, incl = pi > pj, pi >= pj
    levels = int(round(math.log2(L)))

    hps = range(nhp)
    sls = [slice(hp * LANES, (hp + 1) * LANES) for hp in hps]

    def shifted(x0, j, sl):
        return jnp.where(row == 0, carry_ref[j:j + 1, sl], pltpu.roll(x0, 1, 0))

    r0 = [r_ref[:, sl] for sl in sls]
    k0 = [k_ref[:, sl] for sl in sls]
    v0 = [v_ref[:, sl] for sl in sls]
    rs = [shifted(x, 0, sl) for x, sl in zip(r0, sls)]
    ks = [shifted(x, 1, sl) for x, sl in zip(k0, sls)]
    vs = [shifted(x, 2, sl) for x, sl in zip(v0, sls)]
    for hp in hps:
        carry_ref[0:1, sls[hp]] = r0[hp][L - 1:L, :]
        carry_ref[1:2, sls[hp]] = k0[hp][L - 1:L, :]
        carry_ref[2:3, sls[hp]] = v0[hp][L - 1:L, :]
    r = [r0[hp] + (rs[hp] - r0[hp]) * mu_ref[0:1, sls[hp]] for hp in hps]
    k = [k0[hp] + (ks[hp] - k0[hp]) * mu_ref[1:2, sls[hp]] for hp in hps]
    v = [v0[hp] + (vs[hp] - v0[hp]) * mu_ref[2:3, sls[hp]] for hp in hps]
    lw = [lw_ref[:, sl] for sl in sls]
    a_lr = [a_ref[:, sl] for sl in sls]

    def cumsum_rows(x):
        step = 1
        while step < L:
            x = x + jnp.where(row >= step, pltpu.roll(x, step, 0), 0.0)
            step *= 2
        return x

    cum = [cumsum_rows(x) for x in lw]
    g_in = [jnp.exp(x) for x in cum]
    kk0 = [k[hp] * par_ref[0:1, sls[hp]] for hp in hps]
    kk_n2 = [head_sum(x * x) for x in kk0]
    kk = [kk0[hp] / jnp.maximum(jnp.sqrt(kk_n2[hp]), 1e-12) for hp in hps]
    kmod = [k[hp] * (1.0 + (a_lr[hp] - 1.0) * par_ref[1:2, sls[hp]]) for hp in hps]
    xa = [pair((-kk[hp]) * jnp.exp(cum[hp] - lw[hp])).astype(md) for hp in hps]
    xr = [pair(r[hp] * g_in[hp]).astype(md) for hp in hps]
    g_inv = [jnp.exp(-x) for x in cum]
    xb = [pair(kk[hp] * a_lr[hp] * g_inv[hp]).astype(md) for hp in hps]
    xk = [pair(kmod[hp] * g_inv[hp]).astype(md) for hp in hps]
    vp = [pair(x).astype(md) for x in v]
    s0 = [s_ref[hp] for hp in hps]
    s0m = [x.astype(md) for x in s0]
    n_ab = [jnp.where(strict, _dot_nt(xa[hp], xb[hp]), 0.0).astype(md) for hp in hps]
    n_ak = [jnp.where(strict, _dot_nt(xa[hp], xk[hp]), 0.0).astype(md) for hp in hps]
    u = [_dot_nt(xa[hp], s0m[hp]) for hp in hps]
    u = [u[hp] + _dot(n_ak[hp], vp[hp]) for hp in hps]
    npow = n_ab
    for lvl in range(levels):
        u = [u[hp] + _dot(npow[hp], u[hp].astype(md)) for hp in hps]
        if lvl + 1 < levels:
            npow = [_dot(npow[hp], npow[hp]).astype(md) for hp in hps]
    um = [x.astype(md) for x in u]
    p_rb = [jnp.where(incl, _dot_nt(xr[hp], xb[hp]), 0.0).astype(md) for hp in hps]
    p_rk = [jnp.where(incl, _dot_nt(xr[hp], xk[hp]), 0.0).astype(md) for hp in hps]
    y_pair = [_dot_nt(xr[hp], s0m[hp]) + _dot(p_rb[hp], um[hp]) + _dot(p_rk[hp], vp[hp]) for hp in hps]
    for hp in hps:
        s_new = (s0[hp] + _dot_tn(um[hp], xb[hp]) + _dot_tn(vp[hp], xk[hp])) * g_in[hp][L - 1:L, :]
        s_ref[hp] = s_new
        sout_ref[hp] = s_new

    inv_n = 1.0 / HEAD_DIM
    y = [x[0:L, :] + x[L:2 * L, :] for x in y_pair]
    yc = [y[hp] - head_sum(y[hp]) * inv_n for hp in hps]
    var = [head_sum(x * x) * inv_n for x in yc]
    bonus = [head_sum(r[hp] * kmod[hp] * par_ref[2:3, sls[hp]]) for hp in hps]
    for hp in hps:
        sl = sls[hp]
        yn = yc[hp] * lax.rsqrt(var[hp] + GN_EPS) * par_ref[3:4, sl] + par_ref[4:5, sl]
        y_ref[:, sl] = (yn + bonus[hp] * v[hp]) * g_ref[:, sl]


def _wkv(proj, lw, a_lr, gate, prev, mu_rkv, par, s0, n_seq, seq_len, chunk):
    rows = n_seq * seq_len
    dr = lw.shape[1]
    nhp = dr // LANES
    nc = seq_len // chunk

    def col(g):
        return pl.BlockSpec((chunk, dr), lambda b, c: (b * nc + c, g))

    state_spec = pl.BlockSpec((None, nhp, LANES, LANES), lambda b, c: (b, 0, 0, 0))
    return pl.pallas_call(
        functools.partial(_wkv_kernel, chunk=chunk, mm_dtype=BF16 if chunk >= 16 else F32),
        grid=(n_seq, nc),
        in_specs=[col(0), col(1), col(2), col(0), col(0), col(0),
                  pl.BlockSpec((None, 3, dr), lambda b, c: (b, 0, 0)),
                  pl.BlockSpec((3, dr), lambda b, c: (0, 0)),
                  pl.BlockSpec((5, dr), lambda b, c: (0, 0)),
                  state_spec],
        out_specs=[col(0), state_spec],
        out_shape=[jax.ShapeDtypeStruct((rows, dr), F32),
                   jax.ShapeDtypeStruct((n_seq, nhp, LANES, LANES), F32)],
        scratch_shapes=[pltpu.VMEM((nhp, LANES, LANES), F32), pltpu.VMEM((8, dr), F32)],
        compiler_params=_cparams("parallel", "arbitrary"),
        name="wkv",
    )(proj, proj, proj, lw, a_lr, gate, prev, mu_rkv, par, s0)


def _pair_state(s):
    b, h, n, _ = s.shape
    s = s.reshape(b, h // 2, 2, n, n)
    z = jnp.zeros_like(s[:, :, 0])
    top = jnp.concatenate([s[:, :, 0], z], axis=-1)
    bot = jnp.concatenate([z, s[:, :, 1]], axis=-1)
    return jnp.concatenate([top, bot], axis=-2)


def _unpair_state(sp):
    n = HEAD_DIM
    b, hp = sp.shape[:2]
    return jnp.stack([sp[:, :, :n, :n], sp[:, :, n:, n:]], axis=2).reshape(b, hp * 2, n, n)


def _alibi_slope(h, n_heads):
    return 2.0 ** (-8.0 * (h + 1) / n_heads)


def _att_prompt_kernel(slope_ref, q_ref, k_ref, v_ref, y_ref, pk_ref, pv_ref, o_buf, l_buf):
    hp = pl.program_id(0)
    n = pl.program_id(1)
    blk = ATT_BLOCK
    sup = q_ref.shape[0]
    scale = HEAD_DIM ** -0.5

    @pl.when(n == 0)
    def _():
        pk_ref[...] = jnp.zeros_like(pk_ref)
        pv_ref[...] = jnp.zeros_like(pv_ref)

    qi = lax.broadcasted_iota(jnp.int32, (2 * blk, 2 * blk), 0) & (blk - 1)
    ki = lax.broadcasted_iota(jnp.int32, (2 * blk, 2 * blk), 1)
    jd = blk + qi - ki
    top = lax.broadcasted_iota(jnp.int32, (2 * blk, 1), 0) < blk
    head_a = lax.broadcasted_iota(jnp.int32, (1, LANES), 1) < HEAD_DIM
    slope = jnp.where(top, slope_ref[2 * hp], slope_ref[2 * hp + 1])
    not_first = (n > 0) | (ki >= blk)

    def rows(ref, start, dil, base=0):
        if dil == 1:
            return ref[base + start:base + start + blk, :]
        return ref[pl.ds(base + start, blk, stride=dil), :]

    for c, (window, dil) in enumerate(BRANCHES):
        band = (jd >= 0) & (jd <= window // dil)
        bias = slope * (jd * dil).astype(F32)
        nblk = sup // (blk * dil)
        for m in range(nblk):
            for r in range(dil):
                start = m * blk * dil + r
                q = rows(q_ref, start, dil)
                if m > 0:
                    kp, vp = rows(k_ref, start - blk * dil, dil), rows(v_ref, start - blk * dil, dil)
                    valid = band
                else:
                    last = (nblk - 1) * blk * dil + r
                    kp, vp = rows(pk_ref, last, dil), rows(pv_ref, last, dil)
                    valid = band & not_first
                qs = jnp.concatenate([jnp.where(head_a, q, 0.0), jnp.where(head_a, 0.0, q)], axis=0).astype(BF16)
                kw = jnp.concatenate([kp, rows(k_ref, start, dil)], axis=0).astype(BF16)
                vw = jnp.concatenate([vp, rows(v_ref, start, dil)], axis=0).astype(BF16)
                s = jnp.where(valid, _dot_nt(qs, kw) * scale - bias, -jnp.inf)
                mx = jnp.max(s, axis=1, keepdims=True)
                p = jnp.exp(s - mx)
                l = jnp.sum(p, axis=1, keepdims=True)
                o2 = _dot(p.astype(BF16), vw) / l
                lse2 = mx + jnp.log(l)
                o = jnp.where(head_a, o2[0:blk, :], o2[blk:2 * blk, :])
                lse = jnp.where(head_a, lse2[0:blk, :], lse2[blk:2 * blk, :])
                if dil == 1:
                    o_buf[c * sup + start:c * sup + start + blk, :] = o
                    l_buf[c * sup + start:c * sup + start + blk, :] = lse
                else:
                    o_buf[pl.ds(c * sup + start, blk, stride=dil), :] = o
                    l_buf[pl.ds(c * sup + start, blk, stride=dil), :] = lse
    pk_ref[...] = k_ref[...]
    pv_ref[...] = v_ref[...]
    la, lb, lc = l_buf[0:sup, :], l_buf[sup:2 * sup, :], l_buf[2 * sup:3 * sup, :]
    mx = jnp.maximum(jnp.maximum(la, lb), lc)
    wa, wb, wc = jnp.exp(la - mx), jnp.exp(lb - mx), jnp.exp(lc - mx)
    y_ref[...] = (wa * o_buf[0:sup, :] + wb * o_buf[sup:2 * sup, :] + wc * o_buf[2 * sup:3 * sup, :]) / (wa + wb + wc)


def _att_prompt(proj, seq, d_att):
    n_heads = d_att // HEAD_DIM
    nhp = d_att // LANES
    sup = ATT_BLOCK * max(d for _, d in BRANCHES)
    assert seq % sup == 0
    slopes = jnp.asarray([_alibi_slope(h, n_heads) for h in range(n_heads)], F32)

    def col(g):
        return pl.BlockSpec((sup, LANES), lambda h, n: (n, g * nhp + h))

    return pl.pallas_call(
        _att_prompt_kernel,
        grid=(nhp, seq // sup),
        in_specs=[pl.BlockSpec(memory_space=pltpu.SMEM), col(3), col(4), col(5)],
        out_specs=pl.BlockSpec((sup, LANES), lambda h, n: (n, h)),
        out_shape=jax.ShapeDtypeStruct((seq, d_att), F32),
        scratch_shapes=[pltpu.VMEM((sup, LANES), F32), pltpu.VMEM((sup, LANES), F32),
                        pltpu.VMEM((len(BRANCHES) * sup, LANES), F32),
                        pltpu.VMEM((len(BRANCHES) * sup, LANES), F32)],
        compiler_params=_cparams("parallel", "arbitrary"),
        name="att_prompt",
    )(slopes, proj, proj, proj)


def _att_sample_kernel(q_ref, kn_ref, vn_ref, ck_ref, cv_ref, y_ref, *, n_heads):
    t_new = q_ref.shape[0]
    wb = ck_ref.shape[0]
    scale = HEAD_DIM ** -0.5
    rows = 2 * t_new
    row = lax.broadcasted_iota(jnp.int32, (rows, 1), 0)
    t = jnp.where(row < t_new, row, row - t_new)
    delta_c = wb + t - lax.broadcasted_iota(jnp.int32, (rows, wb), 1)
    delta_n = t - lax.broadcasted_iota(jnp.int32, (rows, t_new), 1)
    dist_c, dist_n = delta_c.astype(F32), delta_n.astype(F32)
    lane = lax.broadcasted_iota(jnp.int32, (1, LANES), 1)
    head_a = lane < HEAD_DIM
    for hp in range(n_heads // 2):
        sl = slice(hp * LANES, (hp + 1) * LANES)
        q = q_ref[:, sl]
        qs = jnp.concatenate([jnp.where(head_a, q, 0.0), jnp.where(head_a, 0.0, q)], axis=0)
        kc = ck_ref[:, sl].astype(BF16)
        vc = cv_ref[:, sl].astype(BF16)
        kn, vn = kn_ref[:, sl], vn_ref[:, sl]
        slope = jnp.where(row < t_new, _alibi_slope(2 * hp, n_heads), _alibi_slope(2 * hp + 1, n_heads))
        sc = _dot_nt(qs.astype(BF16), kc) * scale - slope * dist_c
        sn = _dot_nt(qs, kn) * scale - slope * dist_n
        outs, lses = [], []
        for window, dil in BRANCHES:
            sh = int(round(math.log2(dil)))
            span = window // dil
            ok_c = ((delta_c & (dil - 1)) == 0) & ((delta_c >> sh) <= span)
            ok_n = (delta_n >= 0) & ((delta_n & (dil - 1)) == 0) & ((delta_n >> sh) <= span)
            zc = jnp.where(ok_c, sc, -jnp.inf)
            zn = jnp.where(ok_n, sn, -jnp.inf)
            m = jnp.maximum(jnp.max(zc, axis=1, keepdims=True), jnp.max(zn, axis=1, keepdims=True))
            pc, pn = jnp.exp(zc - m), jnp.exp(zn - m)
            l = jnp.sum(pc, axis=1, keepdims=True) + jnp.sum(pn, axis=1, keepdims=True)
            outs.append((_dot(pc.astype(BF16), vc) + _dot(pn, vn)) / l)
            lses.append(m + jnp.log(l))
        m = jnp.maximum(jnp.maximum(lses[0], lses[1]), lses[2])
        ws = [jnp.exp(x - m) for x in lses]
        y2 = (ws[0] * outs[0] + ws[1] * outs[1] + ws[2] * outs[2]) / (ws[0] + ws[1] + ws[2])
        y_ref[:, sl] = jnp.where(head_a, y2[0:t_new, :], y2[t_new:rows, :])


def _att_sample(proj, cache_k, cache_v, n_seq, t_new, d_att):
    wb = cache_k.shape[1]
    ck = cache_k.reshape(n_seq, wb, d_att)
    cv = cache_v.reshape(n_seq, wb, d_att)

    def new(g):
        return pl.BlockSpec((t_new, d_att), lambda b: (b, g))

    cache_spec = pl.BlockSpec((None, wb, d_att), lambda b: (b, 0, 0))
    return pl.pallas_call(
        functools.partial(_att_sample_kernel, n_heads=d_att // HEAD_DIM),
        grid=(n_seq,),
        in_specs=[new(3), new(4), new(5), cache_spec, cache_spec],
        out_specs=pl.BlockSpec((t_new, d_att), lambda b: (b, 0)),
        out_shape=jax.ShapeDtypeStruct((n_seq * t_new, d_att), F32),
        compiler_params=_cparams("parallel"),
        name="att_sample",
    )(proj, proj, proj, ck, cv)


def _outproj_kernel(ya_ref, yb_ref, xn_ref, w_ref, g_ref, b_ref, h_ref, hb_ref, *, alpha):
    half = ya_ref.shape[1]
    mix = _dot(ya_ref[...].astype(BF16), w_ref[0:half, :]) + _dot(yb_ref[...].astype(BF16), w_ref[half:, :])
    h = _layer_norm(alpha * xn_ref[...] + mix, g_ref[...], b_ref[...])
    h_ref[...] = h
    hb_ref[...] = h.astype(BF16)


def _outproj(ya, yb, xn, w_out, g, b, alpha, tm):
    n, d = xn.shape
    row = pl.BlockSpec((tm, d), lambda i: (i, 0))
    half = pl.BlockSpec((tm, ya.shape[1]), lambda i: (i, 0))
    vec = pl.BlockSpec((1, d), lambda i: (0, 0))
    return pl.pallas_call(
        functools.partial(_outproj_kernel, alpha=alpha),
        grid=(n // tm,),
        in_specs=[half, half, row, pl.BlockSpec(w_out.shape, lambda i: (0, 0)), vec, vec],
        out_specs=[row, row],
        out_shape=[jax.ShapeDtypeStruct((n, d), F32), jax.ShapeDtypeStruct((n, d), BF16)],
        compiler_params=_cparams("parallel"),
        name="outproj_ln",
    )(ya, yb, xn, w_out, g, b)


def _top16(s, want_rank, idx=None):
    n, t = s.shape
    if idx is None:
        idx = lax.broadcasted_iota(jnp.int32, (n, t), 0)
    slot = lax.broadcasted_iota(jnp.int32, (PEER_TOPK, t), 0)
    rank = jnp.full((n, t), 99.0, F32)
    vals = jnp.zeros((PEER_TOPK, t), F32)
    picks = jnp.zeros((PEER_TOPK, t), jnp.int32)
    for it in range(PEER_TOPK):
        m = jnp.max(s, axis=0, keepdims=True)
        pick = jnp.min(jnp.where(s == m, idx, 1 << 20), axis=0, keepdims=True)
        hit = idx == pick
        s = jnp.where(hit, -jnp.inf, s)
        vals = jnp.where(slot == it, m, vals)
        if want_rank:
            rank = jnp.where(hit, float(it), rank)
        else:
            picks = jnp.where(slot == it, pick, picks)
    return vals, (rank if want_rank else picks)


def _candidates(v1, v2):
    t = v1.shape[1]
    j8 = lax.broadcasted_iota(jnp.int32, (8, t), 0)
    vals = [v1[0:1, :] + v2[0:8, :], v1[0:1, :] + v2[8:16, :], v1[1:2, :] + v2[0:8, :]]
    idx = [j8, j8 + 8, j8 + PEER_TOPK]
    for i in range(2, 8):
        vals.append(jnp.where(j8 < PEER_TOPK // (i + 1), v1[i:i + 1, :] + v2[0:8, :], -jnp.inf))
        idx.append(j8 + i * PEER_TOPK)
    vals.append(v1[8:16, :] + v2[0:1, :])
    idx.append((j8 + 8) * PEER_TOPK)
    return jnp.concatenate(vals, axis=0), jnp.concatenate(idx, axis=0)


def _peer_select_kernel(q_ref, k1_ref, k2_ref, rank2_ref, e2_ref, len1_ref, e1z_ref):
    half = k1_ref.shape[1]
    tt = q_ref.shape[0]
    qb = q_ref[...].astype(BF16)
    s1_all = _dot_nt(k1_ref[...], qb[:, 0:half])
    s2_all = _dot_nt(k2_ref[...], qb[:, half:2 * half])
    slot = lax.broadcasted_iota(jnp.int32, (PEER_TOPK, LANES), 0)
    for c in range(tt // LANES):
        cs = slice(c * LANES, (c + 1) * LANES)
        s1, s2 = s1_all[:, cs], s2_all[:, cs]
        v1, rank1 = _top16(s1, True)
        v2, rank2 = _top16(s2, True)
        cand, cand_idx = _candidates(v1, v2)
        top_s, picks = _top16(cand, False, cand_idx)
        prow = picks >> 4
        mx = v1[0:1, :] + v2[0:1, :]
        z = jnp.sum(jnp.exp(top_s - mx), axis=0, keepdims=True)
        len1 = jnp.zeros(s1.shape, F32)
        for i in range(PEER_TOPK):
            len_i = jnp.sum(jnp.where(prow == i, 1.0, 0.0), axis=0, keepdims=True)
            len1 = jnp.where(rank1 == float(i), len_i, len1)
        rank2_ref[:, cs] = rank2
        len1_ref[:, cs] = len1
        e2_ref[:, cs] = jnp.exp(s2 - v2[0:1, :])
        e1z_ref[:, cs] = jnp.exp(s1 - v1[0:1, :]) / z


def _peer_select(q, k1, k2, tt):
    n = q.shape[0]
    dk = 2 * k1.shape[1]
    nh = q.shape[1] // dk
    out_spec = pl.BlockSpec((None, N_KEYS, tt), lambda i, h: (h, 0, i))
    shape = jax.ShapeDtypeStruct((nh, N_KEYS, n), F32)
    return pl.pallas_call(
        _peer_select_kernel,
        grid=(n // tt, nh),
        in_specs=[pl.BlockSpec((tt, dk), lambda i, h: (i, h)),
                  pl.BlockSpec(k1.shape, lambda i, h: (0, 0)), pl.BlockSpec(k2.shape, lambda i, h: (0, 0))],
        out_specs=[out_spec] * 4,
        out_shape=[shape] * 4,
        compiler_params=_cparams("parallel", "parallel"),
        name="peer_select",
    )(q, k1, k2)


def _gelu(x):
    return 0.5 * x * (1.0 + lax.erf(x * (2.0 ** -0.5)))


def _peer_dense_kernel(hb_ref, h_ref, rank2_ref, e2_ref, len1_ref, e1z_ref, u_ref, vt_ref, g_ref, b_ref,
                       y_ref, acc_ref, act_ref, *, alpha):
    e = pl.program_id(1)
    eb = u_ref.shape[0]
    nh = rank2_ref.shape[0]

    @pl.when(e == 0)
    def _():
        acc_ref[...] = jnp.zeros_like(acc_ref)

    tt = hb_ref.shape[0]
    sub = 2 * N_KEYS
    total = None
    for s in range(eb // sub):
        ht = _dot_nt(u_ref[s * sub:(s + 1) * sub, :], hb_ref[...])
        for ii in range(sub // N_KEYS):
            i1 = e * (eb // N_KEYS) + s * (sub // N_KEYS) + ii
            rows = slice(ii * N_KEYS, (ii + 1) * N_KEYS)
            len_rows = [len1_ref[hh, pl.ds(i1, 1), :] for hh in range(nh)]
            e1z_rows = [e1z_ref[hh, pl.ds(i1, 1), :] for hh in range(nh)]
            for c in range(tt // LANES):
                cs = slice(c * LANES, (c + 1) * LANES)
                gate = jnp.zeros((N_KEYS, LANES), F32)
                for hh in range(nh):
                    keep = rank2_ref[hh, :, cs] < len_rows[hh][:, cs]
                    gate = gate + e2_ref[hh, :, cs] * jnp.where(keep, e1z_rows[hh][:, cs], 0.0)
                act_ref[s * sub + ii * N_KEYS:s * sub + (ii + 1) * N_KEYS, cs] = (
                    _gelu(ht[rows, cs]) * gate).astype(BF16)
        part = _dot(vt_ref[:, s * sub:(s + 1) * sub], act_ref[s * sub:(s + 1) * sub, :])
        total = part if total is None else total + part
    acc_ref[...] += total

    @pl.when(e == pl.num_programs(1) - 1)
    def _():
        z = alpha * h_ref[...] + acc_ref[...].T
        y_ref[...] = _layer_norm(z, g_ref[...], b_ref[...])


def _peer_dense(hb, h, sel, u_b, vt_b, g, b, alpha, tt, eb):
    n, d = h.shape
    n_exp = u_b.shape[0]
    nh = sel[0].shape[0]
    row = pl.BlockSpec((tt, d), lambda i, e: (i, 0))
    sel_spec = pl.BlockSpec((nh, N_KEYS, tt), lambda i, e: (0, 0, i))
    vec = pl.BlockSpec((1, d), lambda i, e: (0, 0))
    return pl.pallas_call(
        functools.partial(_peer_dense_kernel, alpha=alpha),
        grid=(n // tt, n_exp // eb),
        in_specs=[row, row, sel_spec, sel_spec, sel_spec, sel_spec,
                  pl.BlockSpec((eb, d), lambda i, e: (e, 0)), pl.BlockSpec((d, eb), lambda i, e: (0, e)),
                  vec, vec],
        out_specs=row,
        out_shape=jax.ShapeDtypeStruct((n, d), F32),
        scratch_shapes=[pltpu.VMEM((d, tt), F32), pltpu.VMEM((eb, tt), BF16)],
        compiler_params=_cparams("parallel", "arbitrary"),
        name="peer_dense",
    )(hb, h, *sel, u_b, vt_b, g, b)


def _tile(n, pref):
    t = pref
    while n % t:
        t //= 2
    return t


def _post_mix(xn, y_rwkv, y_att, lp, alpha):
    n = xn.shape[0]
    h, hb = _outproj(y_rwkv, y_att, xn, lp["w_out"], lp["ln1_g"], lp["ln1_b"], alpha, _tile(n, 512))
    q = _matmul(hb, lp["peer_wq"], _tile(n, 1024), 1024)
    sel = _peer_select(q, lp["peer_k1"], lp["peer_k2"], _tile(n, 256))
    return _peer_dense(hb, h, sel, lp["peer_u"], lp["peer_vt"], lp["ln2_g"], lp["ln2_b"], alpha,
                       _tile(n, 512), 512)


def kernel(x_prompt, x_sample, cache_k, cache_v, state_wkv, state_shift, ln_emb_g, ln_emb_b, w_in, mu_rkv, mu_x, w0, w1, w2, a0, a1, a2, g1, g2, k_k, k_a, r_k, lnx_g, lnx_b, w_out, ln1_g, ln1_b, ln2_g, ln2_b, peer_wq, peer_k1, peer_k2, peer_u, peer_v):
    depth = w_in.shape[0]
    bp, seq, d = x_prompt.shape
    bs, t_new, _ = x_sample.shape
    dr = w2.shape[-1]
    d_att = d - dr
    h_att = d_att // HEAD_DIM
    h_rwkv = dr // HEAD_DIM
    alpha = (2.0 * depth) ** 0.25
    keep = min(max(w for w, _ in BRANCHES), seq)
    row2 = lambda a: a.reshape(1, -1)

    xp = x_prompt.reshape(bp * seq, d)
    xs = x_sample.reshape(bs * t_new, d)
    emb_g, emb_b = row2(ln_emb_g), row2(ln_emb_b)
    outs = {k: [] for k in ("ck_p", "cv_p", "wkv_p", "sh_p", "ck_s", "cv_s", "wkv_s", "sh_s")}
    for l in range(depth):
        lp = {
            "mu_x": mu_x[l], "w1": w1[l].astype(BF16), "w2": w2[l].astype(BF16), "w0": row2(w0[l]),
            "a1": a1[l].astype(BF16), "a2": a2[l].astype(BF16), "a0": row2(a0[l]),
            "g1": g1[l].astype(BF16), "g2": g2[l].astype(BF16),
            "w_out": w_out[l].astype(BF16), "ln1_g": row2(ln1_g[l]), "ln1_b": row2(ln1_b[l]),
            "ln2_g": row2(ln2_g[l]), "ln2_b": row2(ln2_b[l]),
            "peer_wq": peer_wq[l].astype(BF16), "peer_k1": peer_k1[l].astype(BF16),
            "peer_k2": peer_k2[l].astype(BF16), "peer_u": peer_u[l].astype(BF16),
            "peer_vt": peer_v[l].astype(BF16).T,
        }
        w_in_b = w_in[l].astype(BF16)
        mu3 = mu_rkv[l].reshape(3, dr)
        par = jnp.stack([k_k[l], k_a[l], r_k[l].reshape(-1), lnx_g[l], lnx_b[l]], axis=0)
        first = l == 0

        xn, xnb, lw, a_lr, gate = _ln_lora(xp, None, seq, first, emb_g, emb_b, lp, _tile(bp * seq, 512))
        proj = _matmul(xnb, w_in_b, _tile(bp * seq, 1024), 1536)
        y_rwkv, s_pair = _wkv(proj, lw, a_lr, gate, jnp.zeros((bp, 3, dr), F32), mu3, par,
                              jnp.zeros((bp, h_rwkv // 2, LANES, LANES), F32), bp, seq, 64)
        att = []
        for b in range(bp):
            pb = proj[b * seq:(b + 1) * seq] if bp > 1 else proj
            att.append(_att_prompt(pb, seq, d_att))
        y_att = att[0] if bp == 1 else jnp.concatenate(att, axis=0)
        proj3 = proj.reshape(bp, seq, -1)
        outs["ck_p"].append(proj3[:, seq - keep:, 4 * dr:5 * dr].reshape(bp, keep, h_att, HEAD_DIM))
        outs["cv_p"].append(proj3[:, seq - keep:, 5 * dr:6 * dr].reshape(bp, keep, h_att, HEAD_DIM))
        outs["wkv_p"].append(_unpair_state(s_pair).astype(state_wkv.dtype))
        outs["sh_p"].append(xn.reshape(bp, seq, d)[:, -1])
        xp = _post_mix(xn, y_rwkv, y_att, lp, alpha)

        init = jnp.repeat(state_shift[l], t_new, axis=0)
        xn, xnb, lw, a_lr, gate = _ln_lora(xs, init, t_new, first, emb_g, emb_b, lp, _tile(bs * t_new, 256))
        proj = _matmul(xnb, w_in_b, _tile(bs * t_new, 256), 1536)
        prev = _matmul(state_shift[l].astype(BF16), w_in_b[:, :3 * dr], bs, 3 * dr // 2)
        y_rwkv, s_pair = _wkv(proj, lw, a_lr, gate, prev.reshape(bs, 3, dr), mu3, par,
                              _pair_state(state_wkv[l].astype(F32)), bs, t_new, t_new)
        y_att = _att_sample(proj, cache_k[l], cache_v[l], bs, t_new, d_att)
        proj3 = proj.reshape(bs, t_new, -1)
        outs["ck_s"].append(proj3[:, :, 4 * dr:5 * dr].reshape(bs, t_new, h_att, HEAD_DIM))
        outs["cv_s"].append(proj3[:, :, 5 * dr:6 * dr].reshape(bs, t_new, h_att, HEAD_DIM))
        outs["wkv_s"].append(_unpair_state(s_pair).astype(state_wkv.dtype))
        outs["sh_s"].append(xn.reshape(bs, t_new, d)[:, -1])
        xs = _post_mix(xn, y_rwkv, y_att, lp, alpha)

    st = lambda k: jnp.stack(outs[k], 0)
    return (xp.reshape(bp, seq, d), xs.reshape(bs, t_new, d), st("ck_p"), st("cv_p"), st("wkv_p"), st("sh_p"),
            st("ck_s"), st("cv_s"), st("wkv_s"), st("sh_s"))
```
